```python
import math
import jax, jax.numpy as jnp
from jax import lax
import numpy as np

D_MODEL = 2048
BATCH = 2
SEQ = 8192
DEPTH = 1

GRID_W = 64
CTX_LEN = 256

DA_HEADS = 8
DA_HEAD_DIM = 64
DA_V = 2 * DA_HEAD_DIM
MLA_HEADS = 8
MLA_Q_RANK = 512
MLA_KV_RANK = 256
MLA_NOPE = 128
MLA_ROPE = 64
MLA_V = 128
MLA_SCALE = (MLA_NOPE + MLA_ROPE) ** -0.5
ROPE_DIM = 64
ROPE_THETA = 10000.0
FFN_HIDDEN = -(-8 * D_MODEL // (3 * 256)) * 256

Q_BLOCK = 128
EPS = 1e-6

DA_Q_W = DA_HEADS * 2 * DA_HEAD_DIM
DA_K_W = DA_HEADS * 2 * DA_HEAD_DIM
DA_V_W = DA_HEADS * DA_V
_IN_SIZES = (DA_Q_W, DA_K_W, DA_V_W, MLA_Q_RANK, MLA_KV_RANK, MLA_ROPE, D_MODEL, D_MODEL)
_IN_SPLITS = tuple(sum(_IN_SIZES[:i + 1]) for i in range(len(_IN_SIZES) - 1))
IN_W = sum(_IN_SIZES)

kernel_name = "hybrid_diffattn_mla_gated_dit_block"


def _rmsnorm(x, g):
    xf = x.astype(jnp.float32)
    y = xf * lax.rsqrt(jnp.mean(xf * xf, axis=-1, keepdims=True) + EPS)
    return (y * g.astype(jnp.float32)).astype(x.dtype)


def _modulate(h, shift, scale):
    return h * (1 + scale) + shift


def _swiglu(h, w_gate, w_up, w_down):
    return (jax.nn.silu(h @ w_gate) * (h @ w_up)) @ w_down


def _rope_tables(n):
    t = jnp.arange(n, dtype=jnp.int32)
    row = (t // GRID_W).astype(jnp.float32)
    col = (t % GRID_W).astype(jnp.float32)
    nf = ROPE_DIM // 4
    inv = ROPE_THETA ** (-jnp.arange(nf, dtype=jnp.float32) / nf)
    ar = row[:, None] * inv
    ac = col[:, None] * inv
    return (jnp.cos(ar), jnp.sin(ar), jnp.cos(ac), jnp.sin(ac))


def _rope_half(x, cos, sin):
    x1, x2 = jnp.split(x.astype(jnp.float32), 2, axis=-1)
    c = cos[None, :, None, :]
    s = sin[None, :, None, :]
    return jnp.concatenate([x1 * c - x2 * s, x2 * c + x1 * s], axis=-1)


def _axial_rope(x, tabs):
    cr, sr, cc, sc = tabs
    xr, xc = jnp.split(x, 2, axis=-1)
    return jnp.concatenate([_rope_half(xr, cr, sr), _rope_half(xc, cc, sc)], axis=-1).astype(x.dtype)


def _sweep_queries(q, block_fn):
    b, n, h, d = q.shape
    nb = n // Q_BLOCK
    qb = q.reshape(b, nb, Q_BLOCK, h, d).transpose(1, 0, 2, 3, 4)
    out = lax.map(block_fn, qb)
    return out.transpose(1, 0, 2, 3, 4).reshape(b, n, out.shape[3], out.shape[4])


def _diff_attend(q, k, v, lam):
    s = jnp.einsum("bqhd,bkhd->bhqk", q, k, preferred_element_type=jnp.float32) * (DA_HEAD_DIM ** -0.5)
    p = jax.nn.softmax(s, axis=-1)
    b, hh, nq, nk = p.shape
    p = p.reshape(b, hh // 2, 2, nq, nk)
    a = p[:, :, 0] - lam * p[:, :, 1]
    return jnp.einsum("bhqk,bkhd->bqhd", a.astype(v.dtype), v)


def _softmax_attend(q, k, v, scale):
    s = jnp.einsum("bqhd,bkhd->bhqk", q, k, preferred_element_type=jnp.float32) * scale
    p = jax.nn.softmax(s, axis=-1)
    return jnp.einsum("bhqk,bkhd->bqhd", p.astype(v.dtype), v)


def _project(h, w_in, mla_q_g, mla_kv_g, w_uq, w_ukv, tabs):
    b, n, _ = h.shape
    z = h @ w_in
    dq, dk, dv, cq, ckv, kr, ga, gb = jnp.split(z, _IN_SPLITS, axis=-1)
    dq = dq.reshape(b, n, 2 * DA_HEADS, DA_HEAD_DIM)
    dk = dk.reshape(b, n, 2 * DA_HEADS, DA_HEAD_DIM)
    dv = dv.reshape(b, n, DA_HEADS, DA_V)
    q = (_rmsnorm(cq, mla_q_g) @ w_uq).reshape(b, n, MLA_HEADS, MLA_NOPE + MLA_ROPE)
    kv = (_rmsnorm(ckv, mla_kv_g) @ w_ukv).reshape(b, n, MLA_HEADS, MLA_NOPE + MLA_V)
    q_nope, q_rope = jnp.split(q, [MLA_NOPE], axis=-1)
    k_nope, mv = jnp.split(kv, [MLA_NOPE], axis=-1)
    k_rope = kr[:, :, None, :]
    if tabs is not None:
        dq = _axial_rope(dq, tabs)
        dk = _axial_rope(dk, tabs)
        q_rope = _axial_rope(q_rope, tabs)
        k_rope = _axial_rope(k_rope, tabs)
    mq = jnp.concatenate([q_nope, q_rope], axis=-1)
    mk = jnp.concatenate([k_nope, jnp.broadcast_to(k_rope, (b, n, MLA_HEADS, MLA_ROPE))], axis=-1)
    return dq, dk, dv, mq, mk, mv, ga, gb


def _merge(o_da, o_mla, ga, gb, da_g, lam_init, w_o_da, w_o_mla, w_out):
    b, n = o_da.shape[:2]
    o_da = _rmsnorm(o_da, da_g) * (1.0 - lam_init)
    y_a = o_da.reshape(b, n, DA_HEADS * DA_V) @ w_o_da
    y_b = o_mla.reshape(b, n, MLA_HEADS * MLA_V) @ w_o_mla
    return (jax.nn.sigmoid(ga) * y_a + jax.nn.sigmoid(gb) * y_b) @ w_out


def setup_inputs(seed: int = 0) -> dict:
    key = jax.random.key(seed)
    ks = jax.random.split(key, 22)
    f32 = jnp.float32

    def w(k, shape, fan_in):
        return jax.random.normal(k, shape, f32) * fan_in ** -0.5

    def gain(k, shape):
        return 1.0 + 0.02 * jax.random.normal(k, shape, f32)

    return {
        "x": jax.random.normal(ks[0], (BATCH, SEQ, D_MODEL), f32),
        "c": jax.random.normal(ks[1], (BATCH, D_MODEL), f32),
        "ctx": jax.random.normal(ks[2], (BATCH, CTX_LEN, D_MODEL), f32),
        "c_ctx": jax.random.normal(ks[3], (D_MODEL,), f32),
        "w_ada": w(ks[4], (DEPTH, D_MODEL, 6 * D_MODEL), D_MODEL),
        "b_ada": 0.01 * jax.random.normal(ks[5], (DEPTH, 6 * D_MODEL), f32),
        "norm1_g": gain(ks[6], (DEPTH, D_MODEL)),
        "norm2_g": gain(ks[7], (DEPTH, D_MODEL)),
        "w_in": w(ks[8], (DEPTH, D_MODEL, IN_W), D_MODEL),
        "da_lambda": 0.1 * jax.random.normal(ks[9], (DEPTH, 4, DA_HEAD_DIM), f32),
        "da_subln_g": gain(ks[10], (DEPTH, DA_V)),
        "mla_q_norm_g": gain(ks[11], (DEPTH, MLA_Q_RANK)),
        "mla_kv_norm_g": gain(ks[12], (DEPTH, MLA_KV_RANK)),
        "w_uq": w(ks[13], (DEPTH, MLA_Q_RANK, MLA_HEADS * (MLA_NOPE + MLA_ROPE)), MLA_Q_RANK),
        "w_ukv": w(ks[14], (DEPTH, MLA_KV_RANK, MLA_HEADS * (MLA_NOPE + MLA_V)), MLA_KV_RANK),
        "w_o_da": w(ks[15], (DEPTH, DA_HEADS * DA_V, D_MODEL), DA_HEADS * DA_V),
        "w_o_mla": w(ks[16], (DEPTH, MLA_HEADS * MLA_V, D_MODEL), MLA_HEADS * MLA_V),
        "w_out": w(ks[17], (DEPTH, D_MODEL, D_MODEL), D_MODEL),
        "w_ffn_gate": w(ks[18], (DEPTH, D_MODEL, FFN_HIDDEN), D_MODEL),
        "w_ffn_up": w(ks[19], (DEPTH, D_MODEL, FFN_HIDDEN), D_MODEL),
        "w_ffn_down": w(ks[20], (DEPTH, FFN_HIDDEN, D_MODEL), FFN_HIDDEN),
        "final_norm_g": gain(ks[21], (D_MODEL,)),
    }


def reference(x, c, ctx, c_ctx, w_ada, b_ada, norm1_g, norm2_g, w_in, da_lambda, da_subln_g,
              mla_q_norm_g, mla_kv_norm_g, w_uq, w_ukv, w_o_da, w_o_mla, w_out,
              w_ffn_gate, w_ffn_up, w_ffn_down, final_norm_g):
    n = x.shape[1]
    tabs = _rope_tables(n)
    for i in range(DEPTH):
        last = i == DEPTH - 1
        lam_init = 0.8 - 0.6 * math.exp(-0.3 * i)
        lp = da_lambda[i].astype(jnp.float32)
        lam = jnp.exp(jnp.sum(lp[0] * lp[1])) - jnp.exp(jnp.sum(lp[2] * lp[3])) + lam_init

        mod = (jax.nn.silu(c) @ w_ada[i] + b_ada[i])[:, None, :]
        mod_c = jax.nn.silu(c_ctx) @ w_ada[i] + b_ada[i]
        sh1, sc1, g1, sh2, sc2, g2 = jnp.split(mod, 6, axis=-1)
        csh1, csc1, cg1, csh2, csc2, cg2 = jnp.split(mod_c, 6, axis=-1)

        h_lat = _modulate(_rmsnorm(x, norm1_g[i]), sh1, sc1)
        h_ctx = _modulate(_rmsnorm(ctx, norm1_g[i]), csh1, csc1)
        dq, dk, dv, mq, mk, mv, ga, gb = _project(
            h_lat, w_in[i], mla_q_norm_g[i], mla_kv_norm_g[i], w_uq[i], w_ukv[i], tabs)
        cdq, cdk, cdv, cmq, cmk, cmv, cga, cgb = _project(
            h_ctx, w_in[i], mla_q_norm_g[i], mla_kv_norm_g[i], w_uq[i], w_ukv[i], None)

        dk_all = jnp.concatenate([cdk, dk], axis=1)
        dv_all = jnp.concatenate([cdv, dv], axis=1)
        mk_all = jnp.concatenate([cmk, mk], axis=1)
        mv_all = jnp.concatenate([cmv, mv], axis=1)
        o_da = _sweep_queries(dq, lambda qb: _diff_attend(qb, dk_all, dv_all, lam))
        o_mla = _sweep_queries(mq, lambda qb: _softmax_attend(qb, mk_all, mv_all, MLA_SCALE))
        x = x + g1 * _merge(o_da, o_mla, ga, gb, da_subln_g[i], lam_init, w_o_da[i], w_o_mla[i], w_out[i])
        x = x + g2 * _swiglu(_modulate(_rmsnorm(x, norm2_g[i]), sh2, sc2),
                             w_ffn_gate[i], w_ffn_up[i], w_ffn_down[i])

        if not last:
            co_da = _diff_attend(cdq, cdk, cdv, lam)
            co_mla = _softmax_attend(cmq, cmk, cmv, MLA_SCALE)
            ctx = ctx + cg1 * _merge(co_da, co_mla, cga, cgb, da_subln_g[i], lam_init,
                                     w_o_da[i], w_o_mla[i], w_out[i])
            ctx = ctx + cg2 * _swiglu(_modulate(_rmsnorm(ctx, norm2_g[i]), csh2, csc2),
                                      w_ffn_gate[i], w_ffn_up[i], w_ffn_down[i])
    return _rmsnorm(x, final_norm_g)
```

```python
import functools
import math

import jax
import jax.numpy as jnp
from jax import lax
from jax.experimental import pallas as pl
from jax.experimental.pallas import tpu as pltpu

F32 = jnp.float32
BF16 = jnp.bfloat16

EPS = 1e-6
GRID_W = 64
ROPE_DIM = 64
ROPE_THETA = 10000.0
DA_HEADS = 8
DA_HEAD_DIM = 64
DA_V = 128
MLA_HEADS = 8
MLA_Q_RANK = 512
MLA_KV_RANK = 256
MLA_NOPE = 128
MLA_ROPE = 64
MLA_V = 128
MLA_SCALE = (MLA_NOPE + MLA_ROPE) ** -0.5
LOG2E = 1.4426950408889634

LANES = 128
MXU_DIM = 256
MLA_QK_PAD = MXU_DIM
VMEM_LIMIT = 56 * 1024 * 1024


def _cparams(sem):
    return pltpu.CompilerParams(dimension_semantics=sem, vmem_limit_bytes=VMEM_LIMIT)


def _rope_block(z, cos, sin):
    lane = lax.broadcasted_iota(jnp.int32, z.shape, 1)
    first = (lane % 32) < 16
    up = pltpu.roll(z, LANES - 16, 1)
    down = pltpu.roll(z, 16, 1)
    return z * cos + jnp.where(first, up, down) * sin


def _rms(x):
    return x * lax.rsqrt(jnp.mean(x * x, axis=-1, keepdims=True) + EPS)


def _ada_kernel(c_ref, w_ref, b_ref, o_ref):
    c = c_ref[...]
    a = (c * jax.nn.sigmoid(c)).astype(BF16)
    o_ref[...] = jnp.dot(a, w_ref[...].astype(BF16), preferred_element_type=F32) + b_ref[...]


def _ada(c8, w, b, tn=1024):
    d, n = w.shape
    return pl.pallas_call(
        _ada_kernel,
        grid=(n // tn,),
        in_specs=[pl.BlockSpec((8, d), lambda j: (0, 0)),
                  pl.BlockSpec((d, tn), lambda j: (0, j)),
                  pl.BlockSpec((1, tn), lambda j: (0, j))],
        out_specs=pl.BlockSpec((8, tn), lambda j: (0, j)),
        out_shape=jax.ShapeDtypeStruct((8, n), F32),
        compiler_params=_cparams(("arbitrary",)),
        name="ada",
    )(c8, w, b)


def _nm_matmul_kernel(x_ref, g_ref, sh_ref, sc_ref, w_ref, *rest, rope, act):
    if rope:
        cos_ref, sin_ref, o_ref, h_scr = rest
    else:
        o_ref, h_scr = rest

    @pl.when(pl.program_id(1) == 0)
    def _():
        y = _rms(x_ref[...]) * g_ref[...]
        h_scr[...] = (y * (1.0 + sc_ref[0]) + sh_ref[0]).astype(BF16)

    z = jnp.dot(h_scr[...], w_ref[...], preferred_element_type=F32)
    if rope:
        cos = cos_ref[...]
        sin = sin_ref[...]
        for c in range(z.shape[1] // LANES):
            sl = slice(c * LANES, (c + 1) * LANES)
            o_ref[:, sl] = _rope_block(z[:, sl], cos, sin).astype(o_ref.dtype)
    else:
        if act == "sigmoid":
            z = jax.nn.sigmoid(z)
        o_ref[...] = z.astype(o_ref.dtype)


def _nm_matmul(x2, g, mod3, sh_idx, sc_idx, w, *, tm, tn, rows_per_mod, mod_row0,
               tabs=None, act=None, out_dtype=BF16):
    m, d = x2.shape
    n = w.shape[1]
    tiles_per_mod = rows_per_mod // tm
    rope = tabs is not None

    def mod_map(k):
        return lambda i, j: (mod_row0 + i // tiles_per_mod, 0, k)

    in_specs = [pl.BlockSpec((tm, d), lambda i, j: (i, 0)),
                pl.BlockSpec((1, d), lambda i, j: (0, 0)),
                pl.BlockSpec((1, 1, d), mod_map(sh_idx)),
                pl.BlockSpec((1, 1, d), mod_map(sc_idx)),
                pl.BlockSpec((d, tn), lambda i, j: (0, j))]
    args = [x2, g, mod3, mod3, w]
    if rope:
        tab_tiles = tabs[0].shape[0] // tm
        tab_spec = pl.BlockSpec((tm, LANES), lambda i, j: (i % tab_tiles, 0))
        in_specs += [tab_spec, tab_spec]
        args += list(tabs)
    return pl.pallas_call(
        functools.partial(_nm_matmul_kernel, rope=rope, act=act),
        grid=(m // tm, n // tn),
        in_specs=in_specs,
        out_specs=pl.BlockSpec((tm, tn), lambda i, j: (i, j)),
        out_shape=jax.ShapeDtypeStruct((m, n), out_dtype),
        scratch_shapes=[pltpu.VMEM((tm, d), BF16)],
        compiler_params=_cparams(("arbitrary", "arbitrary")),
        name="nm_matmul",
    )(*args)


def _mla_q_kernel(z_ref, g_ref, w_ref, cos_ref, sin_ref, o_ref):
    y = (_rms(z_ref[...]) * g_ref[...]).astype(BF16)
    q = jnp.dot(y, w_ref[...], preferred_element_type=F32)
    cos = cos_ref[...]
    sin = sin_ref[...]
    for h in range(MLA_HEADS):
        lo = h * MLA_QK_PAD
        o_ref[:, lo:lo + MLA_NOPE] = q[:, lo:lo + MLA_NOPE].astype(o_ref.dtype)
        blk = q[:, lo + MLA_NOPE:lo + MLA_QK_PAD]
        o_ref[:, lo + MLA_NOPE:lo + MLA_QK_PAD] = _rope_block(blk, cos, sin).astype(o_ref.dtype)


def _mla_q(zc, g, w, tabs, tm):
    m = zc.shape[0]
    n = w.shape[1]
    tab_tiles = tabs[0].shape[0] // tm
    tab_spec = pl.BlockSpec((tm, LANES), lambda i: (i % tab_tiles, 0))
    return pl.pallas_call(
        _mla_q_kernel,
        grid=(m // tm,),
        in_specs=[pl.BlockSpec((tm, MLA_Q_RANK), lambda i: (i, 0)),
                  pl.BlockSpec((1, MLA_Q_RANK), lambda i: (0, 0)),
                  pl.BlockSpec((MLA_Q_RANK, n), lambda i: (0, 0)),
                  tab_spec, tab_spec],
        out_specs=pl.BlockSpec((tm, n), lambda i: (i, 0)),
        out_shape=jax.ShapeDtypeStruct((m, n), BF16),
        compiler_params=_cparams(("arbitrary",)),
        name="mla_q",
    )(zc, g, w, *tabs)


def _mla_kv_kernel(z_ref, g_ref, w_ref, *rest, rope):
    if rope:
        cos_ref, sin_ref, mk_ref, mv_ref = rest
    else:
        mk_ref, mv_ref = rest
    z = z_ref[...]
    y = (_rms(z[:, :MLA_KV_RANK]) * g_ref[...]).astype(BF16)
    kv = jnp.dot(y, w_ref[...], preferred_element_type=F32)
    krb = z[:, MLA_KV_RANK:MLA_KV_RANK + LANES]
    if rope:
        krb = _rope_block(krb, cos_ref[...], sin_ref[...])
    krb = krb.astype(mk_ref.dtype)
    for h in range(MLA_HEADS):
        lo = h * MLA_QK_PAD
        mk_ref[:, lo:lo + MLA_NOPE] = kv[:, h * MLA_NOPE:(h + 1) * MLA_NOPE].astype(mk_ref.dtype)
        mk_ref[:, lo + MLA_NOPE:lo + MLA_QK_PAD] = krb
    mv_ref[...] = kv[:, MLA_HEADS * MLA_NOPE:].astype(mv_ref.dtype)


def _mla_kv(zc, g, w, tabs, tm):
    m = zc.shape[0]
    rope = tabs is not None
    in_specs = [pl.BlockSpec((tm, 512), lambda i: (i, 1)),
                pl.BlockSpec((1, MLA_KV_RANK), lambda i: (0, 0)),
                pl.BlockSpec(w.shape, lambda i: (0, 0))]
    args = [zc, g, w]
    if rope:
        tab_tiles = tabs[0].shape[0] // tm
        tab_spec = pl.BlockSpec((tm, LANES), lambda i: (i % tab_tiles, 0))
        in_specs += [tab_spec, tab_spec]
        args += list(tabs)
    nk = MLA_HEADS * MLA_QK_PAD
    nv = MLA_HEADS * MLA_V
    return pl.pallas_call(
        functools.partial(_mla_kv_kernel, rope=rope),
        grid=(m // tm,),
        in_specs=in_specs,
        out_specs=[pl.BlockSpec((tm, nk), lambda i: (i, 0)),
                   pl.BlockSpec((tm, nv), lambda i: (i, 0))],
        out_shape=[jax.ShapeDtypeStruct((m, nk), BF16),
                   jax.ShapeDtypeStruct((m, nv), BF16)],
        compiler_params=_cparams(("arbitrary",)),
        name="mla_kv",
    )(*args)


def _attn_kernel(q_ref, k_ref, kc_ref, v_ref, vc_ref, *rest, diff, lam_init, tq, tk, n_lat, n_ctx):
    if diff:
        lam_ref, g_ref, o_ref, kall, vt, qe, acc = rest
    else:
        o_ref, kall, vt, qe, acc = rest
    nchunks = (n_lat + n_ctx) // tk
    tqe = qe.shape[0]

    @pl.when(pl.program_id(2) == 0)
    def _stage_keys_values():
        for c in range(nchunks):
            lo = c * tk
            hi = lo + tk
            if hi <= n_lat:
                kall[c] = k_ref[0, lo:hi, :]
                vt[c] = v_ref[0, lo:hi, :].astype(F32).T.astype(BF16)
            else:
                nl = n_lat - lo
                kall[c, :nl, :] = k_ref[0, lo:n_lat, :]
                kall[c, nl:, :] = kc_ref[0]
                vt[c, :, :nl] = v_ref[0, lo:n_lat, :].astype(F32).T.astype(BF16)
                vt[c, :, nl:] = vc_ref[0].astype(F32).T.astype(BF16)

    q = q_ref[0]
    if diff:
        lane = lax.broadcasted_iota(jnp.int32, q.shape, 1)
        zero = jnp.zeros_like(q)
        qe[:tq, :] = jnp.where(lane < DA_HEAD_DIM, q, zero)
        qe[tq:, :] = jnp.where(lane >= DA_HEAD_DIM, q, zero)
    else:
        qe[...] = q
    acc[...] = jnp.zeros_like(acc)

    def body(c, carry):
        m, l = carry
        s = lax.dot_general(kall[c], qe[...], (((1,), (1,)), ((), ())),
                            preferred_element_type=F32)
        m_new = jnp.maximum(m, jnp.max(s, axis=0, keepdims=True))
        alpha = jnp.exp2(m - m_new)
        p = jnp.exp2(s - m_new)
        l_new = alpha * l + jnp.sum(p, axis=0, keepdims=True)
        pv = jnp.dot(vt[c], p.astype(BF16), preferred_element_type=F32)
        acc[...] = acc[...] * alpha + pv
        return m_new, l_new

    m0 = jnp.full((1, tqe), -jnp.inf, F32)
    l0 = jnp.zeros((1, tqe), F32)
    _, l = lax.fori_loop(0, nchunks, body, (m0, l0))

    ot = acc[...] / l
    if diff:
        lp = lam_ref[...]
        lam = (jnp.exp(jnp.sum(lp[0:1] * lp[1:2], axis=-1, keepdims=True))
               - jnp.exp(jnp.sum(lp[2:3] * lp[3:4], axis=-1, keepdims=True)) + lam_init)
        ot = ot[:, :tq] - lam * ot[:, tq:]
        o = _rms(ot.T) * g_ref[...] * (1.0 - lam_init)
    else:
        o = ot.T
    o_ref[0] = o.astype(o_ref.dtype)


def _attention(q, k, kc, v, vc, *, heads, qw, tq, tk, diff, k_col0=0, lam=None, g=None,
               lam_init=0.0):
    b, n_lat, _ = q.shape
    n_ctx = kc.shape[1]
    dv = v.shape[2] // heads
    tqe = 2 * tq if diff else tq
    nchunks = (n_lat + n_ctx) // tk
    in_specs = [pl.BlockSpec((1, tq, qw), lambda bi, h, i: (bi, i, h)),
                pl.BlockSpec((1, n_lat, qw), lambda bi, h, i: (bi, 0, h + k_col0)),
                pl.BlockSpec((1, n_ctx, qw), lambda bi, h, i: (bi, 0, h)),
                pl.BlockSpec((1, n_lat, dv), lambda bi, h, i: (bi, 0, h)),
                pl.BlockSpec((1, n_ctx, dv), lambda bi, h, i: (bi, 0, h))]
    args = [q, k, kc, v, vc]
    if diff:
        in_specs += [pl.BlockSpec(lam.shape, lambda bi, h, i: (0, 0)),
                     pl.BlockSpec(g.shape, lambda bi, h, i: (0, 0))]
        args += [lam, g]
    return pl.pallas_call(
        functools.partial(_attn_kernel, diff=diff, lam_init=lam_init, tq=tq, tk=tk,
                          n_lat=n_lat, n_ctx=n_ctx),
        grid=(b, heads, n_lat // tq),
        in_specs=in_specs,
        out_specs=pl.BlockSpec((1, tq, dv), lambda bi, h, i: (bi, i, h)),
        out_shape=jax.ShapeDtypeStruct((b, n_lat, heads * dv), BF16),
        scratch_shapes=[pltpu.VMEM((nchunks, tk, qw), BF16),
                        pltpu.VMEM((nchunks, dv, tk), BF16),
                        pltpu.VMEM((tqe, qw), BF16),
                        pltpu.VMEM((dv, tqe), F32)],
        compiler_params=_cparams(("arbitrary", "arbitrary", "arbitrary")),
        name="diff_attn" if diff else "mla_attn",
    )(*args)


def _gate_kernel(oa_ref, ob_ref, ga_ref, gb_ref, wa_ref, wb_ref, u_ref):
    ya = jnp.dot(oa_ref[...], wa_ref[...], preferred_element_type=F32)
    yb = jnp.dot(ob_ref[...], wb_ref[...], preferred_element_type=F32)
    u = ga_ref[...].astype(F32) * ya + gb_ref[...].astype(F32) * yb
    u_ref[...] = u.astype(u_ref.dtype)


def _gate(oa, ob, gates, wa, wb, tm, tn):
    m, ka = oa.shape
    n = wa.shape[1]
    nt = n // tn
    return pl.pallas_call(
        _gate_kernel,
        grid=(m // tm, nt),
        in_specs=[pl.BlockSpec((tm, ka), lambda i, j: (i, 0)),
                  pl.BlockSpec((tm, ka), lambda i, j: (i, 0)),
                  pl.BlockSpec((tm, tn), lambda i, j: (i, j)),
                  pl.BlockSpec((tm, tn), lambda i, j: (i, j + nt)),
                  pl.BlockSpec((ka, tn), lambda i, j: (0, j)),
                  pl.BlockSpec((ka, tn), lambda i, j: (0, j))],
        out_specs=pl.BlockSpec((tm, tn), lambda i, j: (i, j)),
        out_shape=jax.ShapeDtypeStruct((m, n), BF16),
        compiler_params=_cparams(("arbitrary", "arbitrary")),
        name="gate_merge",
    )(oa, ob, gates, gates, wa, wb)


def _out_proj_kernel(u_ref, w_ref, x_ref, g_ref, o_ref):
    y = jnp.dot(u_ref[...], w_ref[...], preferred_element_type=F32)
    o_ref[...] = x_ref[...] + g_ref[0] * y


def _out_proj(u, w, x2, mod3, g_idx, rows_per_mod, tm, tn):
    m, d = u.shape
    n = w.shape[1]
    nt = n // tn
    tiles_per_mod = rows_per_mod // tm
    return pl.pallas_call(
        _out_proj_kernel,
        grid=(m // tm, nt),
        in_specs=[pl.BlockSpec((tm, d), lambda i, j: (i, 0)),
                  pl.BlockSpec((d, tn), lambda i, j: (0, j)),
                  pl.BlockSpec((tm, tn), lambda i, j: (i, j)),
                  pl.BlockSpec((1, 1, tn), lambda i, j: (i // tiles_per_mod, 0, g_idx * nt + j))],
        out_specs=pl.BlockSpec((tm, tn), lambda i, j: (i, j)),
        out_shape=jax.ShapeDtypeStruct((m, n), F32),
        compiler_params=_cparams(("arbitrary", "arbitrary")),
        name="out_proj",
    )(u, w, x2, mod3)


def _ffn_kernel(x_ref, g_ref, sh_ref, sc_ref, gate_ref, wg_ref, wu_ref, wd_ref, gf_ref,
                o_ref, h_scr, acc):
    j = pl.program_id(1)

    @pl.when(j == 0)
    def _():
        y = _rms(x_ref[...]) * g_ref[...]
        h_scr[...] = (y * (1.0 + sc_ref[0]) + sh_ref[0]).astype(BF16)
        acc[...] = jnp.zeros_like(acc)

    h = h_scr[...]
    a = jnp.dot(h, wg_ref[...], preferred_element_type=F32)
    u = jnp.dot(h, wu_ref[...], preferred_element_type=F32)
    t = (a * jax.nn.sigmoid(a) * u).astype(BF16)
    acc[...] += jnp.dot(t, wd_ref[...], preferred_element_type=F32)

    @pl.when(j == pl.num_programs(1) - 1)
    def _():
        x2 = x_ref[...] + gate_ref[0] * acc[...]
        o_ref[...] = _rms(x2) * gf_ref[...]


def _ffn(x1, g, mod3, wg, wu, wd, gf, rows_per_mod, tm, th):
    m, d = x1.shape
    hid = wg.shape[1]
    tiles_per_mod = rows_per_mod // tm

    def mod_map(k):
        return lambda i, j: (i // tiles_per_mod, 0, k)

    return pl.pallas_call(
        _ffn_kernel,
        grid=(m // tm, hid // th),
        in_specs=[pl.BlockSpec((tm, d), lambda i, j: (i, 0)),
                  pl.BlockSpec((1, d), lambda i, j: (0, 0)),
                  pl.BlockSpec((1, 1, d), mod_map(3)),
                  pl.BlockSpec((1, 1, d), mod_map(4)),
                  pl.BlockSpec((1, 1, d), mod_map(5)),
                  pl.BlockSpec((d, th), lambda i, j: (0, j)),
                  pl.BlockSpec((d, th), lambda i, j: (0, j)),
                  pl.BlockSpec((th, d), lambda i, j: (j, 0)),
                  pl.BlockSpec((1, d), lambda i, j: (0, 0))],
        out_specs=pl.BlockSpec((tm, d), lambda i, j: (i, 0)),
        out_shape=jax.ShapeDtypeStruct((m, d), F32),
        scratch_shapes=[pltpu.VMEM((tm, d), BF16), pltpu.VMEM((tm, d), F32)],
        compiler_params=_cparams(("arbitrary", "arbitrary")),
        name="ffn",
    )(x1, g, mod3, mod3, mod3, wg, wu, wd, gf)


def _rope_tables(n):
    t = jnp.arange(n, dtype=jnp.int32)
    row = (t // GRID_W).astype(F32)
    col = (t % GRID_W).astype(F32)
    nf = ROPE_DIM // 4
    inv = ROPE_THETA ** (-jnp.arange(nf, dtype=F32) / nf)
    ar = row[:, None] * inv
    ac = col[:, None] * inv
    cr, sr, cc, sc = jnp.cos(ar), jnp.sin(ar), jnp.cos(ac), jnp.sin(ac)
    cos64 = jnp.concatenate([cr, cr, cc, cc], axis=-1)
    sin64 = jnp.concatenate([-sr, sr, -sc, sc], axis=-1)
    return jnp.tile(cos64, (1, LANES // ROPE_DIM)), jnp.tile(sin64, (1, LANES // ROPE_DIM))


def kernel(x, c, ctx, c_ctx, w_ada, b_ada, norm1_g, norm2_g, w_in, da_lambda, da_subln_g,
           mla_q_norm_g, mla_kv_norm_g, w_uq, w_ukv, w_o_da, w_o_mla, w_out,
           w_ffn_gate, w_ffn_up, w_ffn_down, final_norm_g):
    bsz, n, d = x.shape
    n_ctx = ctx.shape[1]
    depth = w_ada.shape[0]
    assert depth == 1, "context-stream update for deeper stacks is not implemented"
    lam_init = 0.8 - 0.6 * math.exp(-0.3 * 0)

    wi = w_in[0]
    o_dq, o_dk, o_dv = 0, 1024, 2048
    o_cq, o_ckv, o_kr, o_ga, o_gb = 3072, 3584, 3840, 3904, 5952
    w_qk = jnp.concatenate([wi[:, o_dq:o_dk] * (DA_HEAD_DIM ** -0.5 * LOG2E), wi[:, o_dk:o_dv]],
                           axis=1).astype(BF16)
    w_k = wi[:, o_dk:o_dv].astype(BF16)
    w_v = wi[:, o_dv:o_cq].astype(BF16)
    w_g = wi[:, o_ga:].astype(BF16)
    w_c = jnp.concatenate([wi[:, o_cq:o_ga], jnp.zeros((d, 1024 - (o_ga - o_cq)), F32)],
                          axis=1).astype(BF16)
    wq = (w_uq[0] * (MLA_SCALE * LOG2E)).reshape(MLA_Q_RANK, MLA_HEADS, MLA_NOPE + MLA_ROPE)
    wq = jnp.pad(wq, ((0, 0), (0, 0), (0, MLA_QK_PAD - MLA_NOPE - MLA_ROPE)))
    wq = wq.reshape(MLA_Q_RANK, MLA_HEADS * MLA_QK_PAD).astype(BF16)
    wkv = w_ukv[0].reshape(MLA_KV_RANK, MLA_HEADS, MLA_NOPE + MLA_V)
    wkv = jnp.concatenate([wkv[:, :, :MLA_NOPE].reshape(MLA_KV_RANK, -1),
                           wkv[:, :, MLA_NOPE:].reshape(MLA_KV_RANK, -1)], axis=1).astype(BF16)
    w_oa = w_o_da[0].astype(BF16)
    w_ob = w_o_mla[0].astype(BF16)
    w_o = w_out[0].astype(BF16)
    w_fg = w_ffn_gate[0].astype(BF16)
    w_fu = w_ffn_up[0].astype(BF16)
    w_fd = w_ffn_down[0].astype(BF16)
    tabs = _rope_tables(n)

    c8 = jnp.zeros((8, d), F32).at[:bsz].set(c).at[bsz].set(c_ctx)
    mod = _ada(c8, w_ada[0], b_ada[0][None, :])
    mod3 = mod.reshape(8, 1, 6 * d)

    x2 = x.reshape(bsz * n, d)
    ctx2 = ctx.reshape(bsz * n_ctx, d)
    g1n = norm1_g[0][None, :]
    tm = 1024
    tmc = bsz * n_ctx

    lat = dict(tm=tm, rows_per_mod=n, mod_row0=0)
    qk = _nm_matmul(x2, g1n, mod3, 0, 1, w_qk, tn=1024, tabs=tabs, **lat)
    dv = _nm_matmul(x2, g1n, mod3, 0, 1, w_v, tn=1024, **lat)
    gates = _nm_matmul(x2, g1n, mod3, 0, 1, w_g, tn=1024, act="sigmoid", **lat)
    zc = _nm_matmul(x2, g1n, mod3, 0, 1, w_c, tn=1024, out_dtype=F32, **lat)
    cx = dict(tm=tmc, rows_per_mod=tmc, mod_row0=bsz)
    cdk = _nm_matmul(ctx2, g1n, mod3, 0, 1, w_k, tn=1024, **cx)
    cdv = _nm_matmul(ctx2, g1n, mod3, 0, 1, w_v, tn=1024, **cx)
    czc = _nm_matmul(ctx2, g1n, mod3, 0, 1, w_c, tn=1024, out_dtype=F32, **cx)

    mq = _mla_q(zc, mla_q_norm_g[0][None, :], wq, tabs, tm)
    mk, mv = _mla_kv(zc, mla_kv_norm_g[0][None, :], wkv, tabs, tm)
    cmk, cmv = _mla_kv(czc, mla_kv_norm_g[0][None, :], wkv, None, tmc)

    qk3 = qk.reshape(bsz, n, 2048)
    o_da = _attention(qk3, qk3, cdk.reshape(bsz, n_ctx, 1024), dv.reshape(bsz, n, 1024),
                      cdv.reshape(bsz, n_ctx, 1024), heads=DA_HEADS, qw=2 * DA_HEAD_DIM,
                      tq=256, tk=768, diff=True, k_col0=DA_HEADS, lam=da_lambda[0],
                      g=da_subln_g[0][None, :],
                      lam_init=lam_init)
    o_mla = _attention(mq.reshape(bsz, n, -1), mk.reshape(bsz, n, -1), cmk.reshape(bsz, n_ctx, -1),
                       mv.reshape(bsz, n, -1), cmv.reshape(bsz, n_ctx, -1), heads=MLA_HEADS,
                       qw=MLA_QK_PAD, tq=512, tk=768, diff=False)

    u = _gate(o_da.reshape(bsz * n, -1), o_mla.reshape(bsz * n, -1), gates, w_oa, w_ob, tm, 1024)
    x1 = _out_proj(u, w_o, x2, mod3, 2, n, tm, 1024)
    out = _ffn(x1, norm2_g[0][None, :], mod3, w_fg, w_fu, w_fd, final_norm_g[None, :], n, 512, 512)
    return out.reshape(bsz, n, d)
```

```python
import functools
import math

import jax
import jax.numpy as jnp
from jax import lax
from jax.experimental import pallas as pl
from jax.experimental.pallas import tpu as pltpu

F32 = jnp.float32
BF16 = jnp.bfloat16

EPS = 1e-6
GRID_W = 64
ROPE_DIM = 64
ROPE_THETA = 10000.0
DA_HEADS = 8
DA_HEAD_DIM = 64
DA_V = 128
MLA_HEADS = 8
MLA_Q_RANK = 512
MLA_KV_RANK = 256
MLA_NOPE = 128
MLA_ROPE = 64
MLA_V = 128
MLA_SCALE = (MLA_NOPE + MLA_ROPE) ** -0.5
LOG2E = 1.4426950408889634

LANES = 128
MXU_DIM = 256
MLA_QK_PAD = MXU_DIM
VMEM_LIMIT = 56 * 1024 * 1024


def _cparams(sem):
    return pltpu.CompilerParams(dimension_semantics=sem, vmem_limit_bytes=VMEM_LIMIT)


def _rope_block(z, cos, sin):
    lane = lax.broadcasted_iota(jnp.int32, z.shape, 1)
    first = (lane % 32) < 16
    up = pltpu.roll(z, LANES - 16, 1)
    down = pltpu.roll(z, 16, 1)
    return z * cos + jnp.where(first, up, down) * sin


def _rms(x):
    return x * lax.rsqrt(jnp.mean(x * x, axis=-1, keepdims=True) + EPS)


def _ada_kernel(c_ref, w_ref, b_ref, o_ref):
    c = c_ref[...]
    a = (c * jax.nn.sigmoid(c)).astype(BF16)
    o_ref[...] = jnp.dot(a, w_ref[...].astype(BF16), preferred_element_type=F32) + b_ref[...]


def _ada(c8, w, b, tn=1024):
    d, n = w.shape
    return pl.pallas_call(
        _ada_kernel,
        grid=(n // tn,),
        in_specs=[pl.BlockSpec((8, d), lambda j: (0, 0)),
                  pl.BlockSpec((d, tn), lambda j: (0, j)),
                  pl.BlockSpec((1, tn), lambda j: (0, j))],
        out_specs=pl.BlockSpec((8, tn), lambda j: (0, j)),
        out_shape=jax.ShapeDtypeStruct((8, n), F32),
        compiler_params=_cparams(("arbitrary",)),
        name="ada",
    )(c8, w, b)


def _nm_matmul_kernel(x_ref, g_ref, sh_ref, sc_ref, w_ref, *rest, rope, act):
    if rope:
        cos_ref, sin_ref, o_ref, h_scr = rest
    else:
        o_ref, h_scr = rest

    @pl.when(pl.program_id(1) == 0)
    def _():
        y = _rms(x_ref[...]) * g_ref[...]
        h_scr[...] = (y * (1.0 + sc_ref[0]) + sh_ref[0]).astype(BF16)

    z = jnp.dot(h_scr[...], w_ref[...], preferred_element_type=F32)
    if rope:
        cos = cos_ref[...]
        sin = sin_ref[...]
        for c in range(z.shape[1] // LANES):
            sl = slice(c * LANES, (c + 1) * LANES)
            o_ref[:, sl] = _rope_block(z[:, sl], cos, sin).astype(o_ref.dtype)
    else:
        if act == "sigmoid":
            z = jax.nn.sigmoid(z)
        o_ref[...] = z.astype(o_ref.dtype)


def _nm_matmul(x2, g, mod3, sh_idx, sc_idx, w, *, tm, tn, rows_per_mod, mod_row0,
               tabs=None, act=None, out_dtype=BF16):
    m, d = x2.shape
    n = w.shape[1]
    tiles_per_mod = rows_per_mod // tm
    rope = tabs is not None

    def mod_map(k):
        return lambda i, j: (mod_row0 + i // tiles_per_mod, 0, k)

    in_specs = [pl.BlockSpec((tm, d), lambda i, j: (i, 0)),
                pl.BlockSpec((1, d), lambda i, j: (0, 0)),
                pl.BlockSpec((1, 1, d), mod_map(sh_idx)),
                pl.BlockSpec((1, 1, d), mod_map(sc_idx)),
                pl.BlockSpec((d, tn), lambda i, j: (0, j))]
    args = [x2, g, mod3, mod3, w]
    if rope:
        tab_tiles = tabs[0].shape[0] // tm
        tab_spec = pl.BlockSpec((tm, LANES), lambda i, j: (i % tab_tiles, 0))
        in_specs += [tab_spec, tab_spec]
        args += list(tabs)
    return pl.pallas_call(
        functools.partial(_nm_matmul_kernel, rope=rope, act=act),
        grid=(m // tm, n // tn),
        in_specs=in_specs,
        out_specs=pl.BlockSpec((tm, tn), lambda i, j: (i, j)),
        out_shape=jax.ShapeDtypeStruct((m, n), out_dtype),
        scratch_shapes=[pltpu.VMEM((tm, d), BF16)],
        compiler_params=_cparams(("arbitrary", "arbitrary")),
        name="nm_matmul",
    )(*args)


def _mla_q_kernel(z_ref, g_ref, w_ref, cos_ref, sin_ref, o_ref):
    y = (_rms(z_ref[...]) * g_ref[...]).astype(BF16)
    q = jnp.dot(y, w_ref[...], preferred_element_type=F32)
    cos = cos_ref[...]
    sin = sin_ref[...]
    for h in range(MLA_HEADS):
        lo = h * MLA_QK_PAD
        o_ref[:, lo:lo + MLA_NOPE] = q[:, lo:lo + MLA_NOPE].astype(o_ref.dtype)
        blk = q[:, lo + MLA_NOPE:lo + MLA_QK_PAD]
        o_ref[:, lo + MLA_NOPE:lo + MLA_QK_PAD] = _rope_block(blk, cos, sin).astype(o_ref.dtype)


def _mla_q(zc, g, w, tabs, tm):
    m = zc.shape[0]
    n = w.shape[1]
    tab_tiles = tabs[0].shape[0] // tm
    tab_spec = pl.BlockSpec((tm, LANES), lambda i: (i % tab_tiles, 0))
    return pl.pallas_call(
        _mla_q_kernel,
        grid=(m // tm,),
        in_specs=[pl.BlockSpec((tm, MLA_Q_RANK), lambda i: (i, 0)),
                  pl.BlockSpec((1, MLA_Q_RANK), lambda i: (0, 0)),
                  pl.BlockSpec((MLA_Q_RANK, n), lambda i: (0, 0)),
                  tab_spec, tab_spec],
        out_specs=pl.BlockSpec((tm, n), lambda i: (i, 0)),
        out_shape=jax.ShapeDtypeStruct((m, n), BF16),
        compiler_params=_cparams(("arbitrary",)),
        name="mla_q",
    )(zc, g, w, *tabs)


def _mla_kv_kernel(z_ref, g_ref, w_ref, *rest, rope):
    if rope:
        cos_ref, sin_ref, mk_ref, mv_ref = rest
    else:
        mk_ref, mv_ref = rest
    z = z_ref[...]
    y = (_rms(z[:, :MLA_KV_RANK]) * g_ref[...]).astype(BF16)
    kv = jnp.dot(y, w_ref[...], preferred_element_type=F32)
    krb = z[:, MLA_KV_RANK:MLA_KV_RANK + LANES]
    if rope:
        krb = _rope_block(krb, cos_ref[...], sin_ref[...])
    krb = krb.astype(mk_ref.dtype)
    for h in range(MLA_HEADS):
        lo = h * MLA_QK_PAD
        mk_ref[:, lo:lo + MLA_NOPE] = kv[:, h * MLA_NOPE:(h + 1) * MLA_NOPE].astype(mk_ref.dtype)
        mk_ref[:, lo + MLA_NOPE:lo + MLA_QK_PAD] = krb
    mv_ref[...] = kv[:, MLA_HEADS * MLA_NOPE:].astype(mv_ref.dtype)


def _mla_kv(zc, g, w, tabs, tm):
    m = zc.shape[0]
    rope = tabs is not None
    in_specs = [pl.BlockSpec((tm, 512), lambda i: (i, 1)),
                pl.BlockSpec((1, MLA_KV_RANK), lambda i: (0, 0)),
                pl.BlockSpec(w.shape, lambda i: (0, 0))]
    args = [zc, g, w]
    if rope:
        tab_tiles = tabs[0].shape[0] // tm
        tab_spec = pl.BlockSpec((tm, LANES), lambda i: (i % tab_tiles, 0))
        in_specs += [tab_spec, tab_spec]
        args += list(tabs)
    nk = MLA_HEADS * MLA_QK_PAD
    nv = MLA_HEADS * MLA_V
    return pl.pallas_call(
        functools.partial(_mla_kv_kernel, rope=rope),
        grid=(m // tm,),
        in_specs=in_specs,
        out_specs=[pl.BlockSpec((tm, nk), lambda i: (i, 0)),
                   pl.BlockSpec((tm, nv), lambda i: (i, 0))],
        out_shape=[jax.ShapeDtypeStruct((m, nk), BF16),
                   jax.ShapeDtypeStruct((m, nv), BF16)],
        compiler_params=_cparams(("arbitrary",)),
        name="mla_kv",
    )(*args)


def _attn_kernel(q_ref, k_ref, kc_ref, v_ref, vc_ref, *rest, diff, lam_init, tq, tk, n_lat, n_ctx):
    if diff:
        lam_ref, g_ref, o_ref, kall, vt, qe, acc, s_a, s_b = rest
    else:
        o_ref, kall, vt, qe, acc, s_a, s_b = rest
    nchunks = (n_lat + n_ctx) // tk
    tqe = qe.shape[0]

    @pl.when(pl.program_id(2) == 0)
    def _stage_keys_values():
        for c in range(nchunks):
            lo = c * tk
            hi = lo + tk
            if hi <= n_lat:
                kall[c] = k_ref[0, lo:hi, :]
                vt[c] = v_ref[0, lo:hi, :].astype(F32).T.astype(BF16)
            else:
                nl = n_lat - lo
                kall[c, :nl, :] = k_ref[0, lo:n_lat, :]
                kall[c, nl:, :] = kc_ref[0]
                vt[c, :, :nl] = v_ref[0, lo:n_lat, :].astype(F32).T.astype(BF16)
                vt[c, :, nl:] = vc_ref[0].astype(F32).T.astype(BF16)

    q = q_ref[0]
    if diff:
        lane = lax.broadcasted_iota(jnp.int32, q.shape, 1)
        zero = jnp.zeros_like(q)
        qe[:tq, :] = jnp.where(lane < DA_HEAD_DIM, q, zero)
        qe[tq:, :] = jnp.where(lane >= DA_HEAD_DIM, q, zero)
    else:
        qe[...] = q
    acc[...] = jnp.zeros_like(acc)

    def scores(c, s_ref):
        s = lax.dot_general(kall[c], qe[...], (((1,), (1,)), ((), ())),
                            preferred_element_type=F32)
        s_ref[...] = s
        return jnp.max(s, axis=0, keepdims=True)

    def softmax_pv(c, s_ref, mc, m, l):
        m_new = jnp.maximum(m, mc)
        alpha = jnp.exp2(m - m_new)
        p = jnp.exp2(s_ref[...] - m_new)
        l_new = alpha * l + jnp.sum(p, axis=0, keepdims=True)
        pv = jnp.dot(vt[c], p.astype(BF16), preferred_element_type=F32)
        acc[...] = acc[...] * alpha + pv
        return m_new, l_new

    m = jnp.full((1, tqe), -jnp.inf, F32)
    l = jnp.zeros((1, tqe), F32)
    mc = scores(0, s_a)
    for c in range(nchunks):
        s_cur, s_next = (s_a, s_b) if c % 2 == 0 else (s_b, s_a)
        mc_next = scores(c + 1, s_next) if c + 1 < nchunks else None
        m, l = softmax_pv(c, s_cur, mc, m, l)
        mc = mc_next

    ot = acc[...] / l
    if diff:
        lp = lam_ref[...]
        lam = (jnp.exp(jnp.sum(lp[0:1] * lp[1:2], axis=-1, keepdims=True))
               - jnp.exp(jnp.sum(lp[2:3] * lp[3:4], axis=-1, keepdims=True)) + lam_init)
        ot = ot[:, :tq] - lam * ot[:, tq:]
        o = _rms(ot.T) * g_ref[...] * (1.0 - lam_init)
    else:
        o = ot.T
    o_ref[0] = o.astype(o_ref.dtype)


def _attention(q, k, kc, v, vc, *, heads, qw, tq, tk, diff, k_col0=0, lam=None, g=None,
               lam_init=0.0):
    b, n_lat, _ = q.shape
    n_ctx = kc.shape[1]
    dv = v.shape[2] // heads
    tqe = 2 * tq if diff else tq
    nchunks = (n_lat + n_ctx) // tk
    in_specs = [pl.BlockSpec((1, tq, qw), lambda bi, h, i: (bi, i, h)),
                pl.BlockSpec((1, n_lat, qw), lambda bi, h, i: (bi, 0, h + k_col0)),
                pl.BlockSpec((1, n_ctx, qw), lambda bi, h, i: (bi, 0, h)),
                pl.BlockSpec((1, n_lat, dv), lambda bi, h, i: (bi, 0, h)),
                pl.BlockSpec((1, n_ctx, dv), lambda bi, h, i: (bi, 0, h))]
    args = [q, k, kc, v, vc]
    if diff:
        in_specs += [pl.BlockSpec(lam.shape, lambda bi, h, i: (0, 0)),
                     pl.BlockSpec(g.shape, lambda bi, h, i: (0, 0))]
        args += [lam, g]
    return pl.pallas_call(
        functools.partial(_attn_kernel, diff=diff, lam_init=lam_init, tq=tq, tk=tk,
                          n_lat=n_lat, n_ctx=n_ctx),
        grid=(b, heads, n_lat // tq),
        in_specs=in_specs,
        out_specs=pl.BlockSpec((1, tq, dv), lambda bi, h, i: (bi, i, h)),
        out_shape=jax.ShapeDtypeStruct((b, n_lat, heads * dv), BF16),
        scratch_shapes=[pltpu.VMEM((nchunks, tk, qw), BF16),
                        pltpu.VMEM((nchunks, dv, tk), BF16),
                        pltpu.VMEM((tqe, qw), BF16),
                        pltpu.VMEM((dv, tqe), F32),
                        pltpu.VMEM((tk, tqe), F32),
                        pltpu.VMEM((tk, tqe), F32)],
        compiler_params=_cparams(("arbitrary", "arbitrary", "arbitrary")),
        name="diff_attn" if diff else "mla_attn",
    )(*args)


def _gate_kernel(oa_ref, ob_ref, ga_ref, gb_ref, wa_ref, wb_ref, u_ref):
    ya = jnp.dot(oa_ref[...], wa_ref[...], preferred_element_type=F32)
    yb = jnp.dot(ob_ref[...], wb_ref[...], preferred_element_type=F32)
    u = ga_ref[...].astype(F32) * ya + gb_ref[...].astype(F32) * yb
    u_ref[...] = u.astype(u_ref.dtype)


def _gate(oa, ob, gates, wa, wb, tm, tn):
    m, ka = oa.shape
    n = wa.shape[1]
    nt = n // tn
    return pl.pallas_call(
        _gate_kernel,
        grid=(m // tm, nt),
        in_specs=[pl.BlockSpec((tm, ka), lambda i, j: (i, 0)),
                  pl.BlockSpec((tm, ka), lambda i, j: (i, 0)),
                  pl.BlockSpec((tm, tn), lambda i, j: (i, j)),
                  pl.BlockSpec((tm, tn), lambda i, j: (i, j + nt)),
                  pl.BlockSpec((ka, tn), lambda i, j: (0, j)),
                  pl.BlockSpec((ka, tn), lambda i, j: (0, j))],
        out_specs=pl.BlockSpec((tm, tn), lambda i, j: (i, j)),
        out_shape=jax.ShapeDtypeStruct((m, n), BF16),
        compiler_params=_cparams(("arbitrary", "arbitrary")),
        name="gate_merge",
    )(oa, ob, gates, gates, wa, wb)


def _out_proj_kernel(u_ref, w_ref, x_ref, g_ref, o_ref):
    y = jnp.dot(u_ref[...], w_ref[...], preferred_element_type=F32)
    o_ref[...] = x_ref[...] + g_ref[0] * y


def _out_proj(u, w, x2, mod3, g_idx, rows_per_mod, tm, tn):
    m, d = u.shape
    n = w.shape[1]
    nt = n // tn
    tiles_per_mod = rows_per_mod // tm
    return pl.pallas_call(
        _out_proj_kernel,
        grid=(m // tm, nt),
        in_specs=[pl.BlockSpec((tm, d), lambda i, j: (i, 0)),
                  pl.BlockSpec((d, tn), lambda i, j: (0, j)),
                  pl.BlockSpec((tm, tn), lambda i, j: (i, j)),
                  pl.BlockSpec((1, 1, tn), lambda i, j: (i // tiles_per_mod, 0, g_idx * nt + j))],
        out_specs=pl.BlockSpec((tm, tn), lambda i, j: (i, j)),
        out_shape=jax.ShapeDtypeStruct((m, n), F32),
        compiler_params=_cparams(("arbitrary", "arbitrary")),
        name="out_proj",
    )(u, w, x2, mod3)


def _ffn_kernel(x_ref, g_ref, sh_ref, sc_ref, gate_ref, wg_ref, wu_ref, wd_ref, gf_ref,
                o_ref, h_scr, acc):
    j = pl.program_id(1)

    @pl.when(j == 0)
    def _():
        y = _rms(x_ref[...]) * g_ref[...]
        h_scr[...] = (y * (1.0 + sc_ref[0]) + sh_ref[0]).astype(BF16)
        acc[...] = jnp.zeros_like(acc)

    h = h_scr[...]
    a = jnp.dot(h, wg_ref[...], preferred_element_type=F32)
    u = jnp.dot(h, wu_ref[...], preferred_element_type=F32)
    t = (a * jax.nn.sigmoid(a) * u).astype(BF16)
    acc[...] += jnp.dot(t, wd_ref[...], preferred_element_type=F32)

    @pl.when(j == pl.num_programs(1) - 1)
    def _():
        x2 = x_ref[...] + gate_ref[0] * acc[...]
        o_ref[...] = _rms(x2) * gf_ref[...]


def _ffn(x1, g, mod3, wg, wu, wd, gf, rows_per_mod, tm, th):
    m, d = x1.shape
    hid = wg.shape[1]
    tiles_per_mod = rows_per_mod // tm

    def mod_map(k):
        return lambda i, j: (i // tiles_per_mod, 0, k)

    return pl.pallas_call(
        _ffn_kernel,
        grid=(m // tm, hid // th),
        in_specs=[pl.BlockSpec((tm, d), lambda i, j: (i, 0)),
                  pl.BlockSpec((1, d), lambda i, j: (0, 0)),
                  pl.BlockSpec((1, 1, d), mod_map(3)),
                  pl.BlockSpec((1, 1, d), mod_map(4)),
                  pl.BlockSpec((1, 1, d), mod_map(5)),
                  pl.BlockSpec((d, th), lambda i, j: (0, j)),
                  pl.BlockSpec((d, th), lambda i, j: (0, j)),
                  pl.BlockSpec((th, d), lambda i, j: (j, 0)),
                  pl.BlockSpec((1, d), lambda i, j: (0, 0))],
        out_specs=pl.BlockSpec((tm, d), lambda i, j: (i, 0)),
        out_shape=jax.ShapeDtypeStruct((m, d), F32),
        scratch_shapes=[pltpu.VMEM((tm, d), BF16), pltpu.VMEM((tm, d), F32)],
        compiler_params=_cparams(("arbitrary", "arbitrary")),
        name="ffn",
    )(x1, g, mod3, mod3, mod3, wg, wu, wd, gf)


def _rope_tables(n):
    t = jnp.arange(n, dtype=jnp.int32)
    row = (t // GRID_W).astype(F32)
    col = (t % GRID_W).astype(F32)
    nf = ROPE_DIM // 4
    inv = ROPE_THETA ** (-jnp.arange(nf, dtype=F32) / nf)
    ar = row[:, None] * inv
    ac = col[:, None] * inv
    cr, sr, cc, sc = jnp.cos(ar), jnp.sin(ar), jnp.cos(ac), jnp.sin(ac)
    cos64 = jnp.concatenate([cr, cr, cc, cc], axis=-1)
    sin64 = jnp.concatenate([-sr, sr, -sc, sc], axis=-1)
    return jnp.tile(cos64, (1, LANES // ROPE_DIM)), jnp.tile(sin64, (1, LANES // ROPE_DIM))


def kernel(x, c, ctx, c_ctx, w_ada, b_ada, norm1_g, norm2_g, w_in, da_lambda, da_subln_g,
           mla_q_norm_g, mla_kv_norm_g, w_uq, w_ukv, w_o_da, w_o_mla, w_out,
           w_ffn_gate, w_ffn_up, w_ffn_down, final_norm_g):
    bsz, n, d = x.shape
    n_ctx = ctx.shape[1]
    depth = w_ada.shape[0]
    assert depth == 1, "context-stream update for deeper stacks is not implemented"
    lam_init = 0.8 - 0.6 * math.exp(-0.3 * 0)

    wi = w_in[0]
    o_dq, o_dk, o_dv = 0, 1024, 2048
    o_cq, o_ckv, o_kr, o_ga, o_gb = 3072, 3584, 3840, 3904, 5952
    w_qk = jnp.concatenate([wi[:, o_dq:o_dk] * (DA_HEAD_DIM ** -0.5 * LOG2E), wi[:, o_dk:o_dv]],
                           axis=1).astype(BF16)
    w_k = wi[:, o_dk:o_dv].astype(BF16)
    w_v = wi[:, o_dv:o_cq].astype(BF16)
    w_g = wi[:, o_ga:].astype(BF16)
    w_c = jnp.concatenate([wi[:, o_cq:o_ga], jnp.zeros((d, 1024 - (o_ga - o_cq)), F32)],
                          axis=1).astype(BF16)
    wq = (w_uq[0] * (MLA_SCALE * LOG2E)).reshape(MLA_Q_RANK, MLA_HEADS, MLA_NOPE + MLA_ROPE)
    wq = jnp.pad(wq, ((0, 0), (0, 0), (0, MLA_QK_PAD - MLA_NOPE - MLA_ROPE)))
    wq = wq.reshape(MLA_Q_RANK, MLA_HEADS * MLA_QK_PAD).astype(BF16)
    wkv = w_ukv[0].reshape(MLA_KV_RANK, MLA_HEADS, MLA_NOPE + MLA_V)
    wkv = jnp.concatenate([wkv[:, :, :MLA_NOPE].reshape(MLA_KV_RANK, -1),
                           wkv[:, :, MLA_NOPE:].reshape(MLA_KV_RANK, -1)], axis=1).astype(BF16)
    w_oa = w_o_da[0].astype(BF16)
    w_ob = w_o_mla[0].astype(BF16)
    w_o = w_out[0].astype(BF16)
    w_fg = w_ffn_gate[0].astype(BF16)
    w_fu = w_ffn_up[0].astype(BF16)
    w_fd = w_ffn_down[0].astype(BF16)
    tabs = _rope_tables(n)

    c8 = jnp.zeros((8, d), F32).at[:bsz].set(c).at[bsz].set(c_ctx)
    mod = _ada(c8, w_ada[0], b_ada[0][None, :])
    mod3 = mod.reshape(8, 1, 6 * d)

    x2 = x.reshape(bsz * n, d)
    ctx2 = ctx.reshape(bsz * n_ctx, d)
    g1n = norm1_g[0][None, :]
    tm = 1024
    tmc = bsz * n_ctx

    lat = dict(tm=tm, rows_per_mod=n, mod_row0=0)
    qk = _nm_matmul(x2, g1n, mod3, 0, 1, w_qk, tn=1024, tabs=tabs, **lat)
    dv = _nm_matmul(x2, g1n, mod3, 0, 1, w_v, tn=1024, **lat)
    gates = _nm_matmul(x2, g1n, mod3, 0, 1, w_g, tn=1024, act="sigmoid", **lat)
    zc = _nm_matmul(x2, g1n, mod3, 0, 1, w_c, tn=1024, out_dtype=F32, **lat)
    cx = dict(tm=tmc, rows_per_mod=tmc, mod_row0=bsz)
    cdk = _nm_matmul(ctx2, g1n, mod3, 0, 1, w_k, tn=1024, **cx)
    cdv = _nm_matmul(ctx2, g1n, mod3, 0, 1, w_v, tn=1024, **cx)
    czc = _nm_matmul(ctx2, g1n, mod3, 0, 1, w_c, tn=1024, out_dtype=F32, **cx)

    mq = _mla_q(zc, mla_q_norm_g[0][None, :], wq, tabs, tm)
    mk, mv = _mla_kv(zc, mla_kv_norm_g[0][None, :], wkv, tabs, tm)
    cmk, cmv = _mla_kv(czc, mla_kv_norm_g[0][None, :], wkv, None, tmc)

    qk3 = qk.reshape(bsz, n, 2048)
    o_da = _attention(qk3, qk3, cdk.reshape(bsz, n_ctx, 1024), dv.reshape(bsz, n, 1024),
                      cdv.reshape(bsz, n_ctx, 1024), heads=DA_HEADS, qw=2 * DA_HEAD_DIM,
                      tq=512, tk=768, diff=True, k_col0=DA_HEADS, lam=da_lambda[0],
                      g=da_subln_g[0][None, :],
                      lam_init=lam_init)
    o_mla = _attention(mq.reshape(bsz, n, -1), mk.reshape(bsz, n, -1), cmk.reshape(bsz, n_ctx, -1),
                       mv.reshape(bsz, n, -1), cmv.reshape(bsz, n_ctx, -1), heads=MLA_HEADS,
                       qw=MLA_QK_PAD, tq=1024, tk=768, diff=False)

    u = _gate(o_da.reshape(bsz * n, -1), o_mla.reshape(bsz * n, -1), gates, w_oa, w_ob, tm, 1024)
    x1 = _out_proj(u, w_o, x2, mod3, 2, n, tm, 1024)
    out = _ffn(x1, norm2_g[0][None, :], mod3, w_fg, w_fu, w_fd, final_norm_g[None, :], n, 512, 512)
    return out.reshape(bsz, n, d)
```
